```python
import math, functools
import jax, jax.numpy as jnp
from jax import lax
import numpy as np

D_MODEL = 4096
BATCH = 1
SEQ = 8192
DEPTH = 1
DEC_BATCH = 128
DEC_SEQ = 8
PAST_LEN = 2048
PAGE_SIZE = 128

D_MIX = D_MODEL
ATTN_HEADS = 16
KV_HEADS = 4
HEAD_DIM = 128
Q_GROUP = ATTN_HEADS // KV_HEADS
DILATED_GROUPS = ((128, 1), (512, 4), (2048, 16))
LOCAL_KEYS = 128
MAX_WINDOW = 2048
ATTN_BLOCK = 128
ROPE_THETA = 10000.0
SSM_HEADS = 32
SSM_HEAD_DIM = 64
D_SSM = SSM_HEADS * SSM_HEAD_DIM
SSM_GROUPS = 4
D_STATE = 128
CONV_WIDTH = 4
CONV_DIM = D_SSM + 2 * SSM_GROUPS * D_STATE
SSD_CHUNK = 128
D_IN_PROJ = ATTN_HEADS * HEAD_DIM + 2 * KV_HEADS * HEAD_DIM + D_SSM + CONV_DIM + SSM_HEADS
PEER_HEADS = 8
N_KEYS = 128
N_EXPERTS = N_KEYS * N_KEYS
D_KEY = 256
PEER_TOPK = 16
PEER_BLOCK = 64
EPS = 1e-6

kernel_name = "hymba_dilated_ssd_peer_step"


def _rmsnorm(x, w):
    xf = x.astype(jnp.float32)
    y = xf * lax.rsqrt(jnp.mean(xf * xf, axis=-1, keepdims=True) + EPS)
    return (y * w.astype(jnp.float32)).astype(x.dtype)


def _rope(x, pos):
    half = x.shape[-1] // 2
    inv = ROPE_THETA ** (-jnp.arange(half, dtype=jnp.float32) / half)
    ang = pos.astype(jnp.float32)[:, None] * inv[None, :]
    cos, sin = jnp.cos(ang)[:, None, :], jnp.sin(ang)[:, None, :]
    xf = x.astype(jnp.float32)
    x1, x2 = xf[..., :half], xf[..., half:]
    return jnp.concatenate([x1 * cos - x2 * sin, x2 * cos + x1 * sin], axis=-1).astype(x.dtype)


def _masked_attend(s, v, mask, spec):
    s = jnp.where(mask, s, -jnp.inf)
    m = jnp.max(s, axis=-1, keepdims=True)
    p = jnp.exp(s - m)
    l = jnp.sum(p, axis=-1)
    o = jnp.einsum(spec, p.astype(v.dtype), v).astype(jnp.float32) / l[..., None]
    return o, m[..., 0] + jnp.log(l)


def _merge_by_denominator(outs, lses):
    w = jax.nn.softmax(jnp.stack(lses, axis=0), axis=0)
    return jnp.sum(w[..., None] * jnp.stack(outs, axis=0), axis=0)


def _band_attend(q, k, v):
    b, L = q.shape[:2]
    nb = -(-L // ATTN_BLOCK)
    pad = nb * ATTN_BLOCK - L

    def padl(t):
        return jnp.pad(t, [(0, 0), (0, pad)] + [(0, 0)] * (t.ndim - 2))

    qb = padl(q).reshape(b, nb, ATTN_BLOCK, KV_HEADS, Q_GROUP, HEAD_DIM)

    def with_prev(t):
        tb = padl(t).reshape(b, nb, ATTN_BLOCK, KV_HEADS, HEAD_DIM)
        prev = jnp.pad(tb[:, :-1], [(0, 0), (1, 0), (0, 0), (0, 0), (0, 0)])
        return jnp.concatenate([prev, tb], axis=2)

    kb, vb = with_prev(k), with_prev(v)
    s = jnp.einsum('bnqkgd,bnskd->bnkgqs', qb, kb).astype(jnp.float32) * (HEAD_DIM ** -0.5)
    qi = jnp.arange(ATTN_BLOCK)[:, None]
    si = jnp.arange(2 * ATTN_BLOCK)[None, :]
    dist = qi + ATTN_BLOCK - si
    kpos = jnp.arange(nb)[:, None, None] * ATTN_BLOCK - ATTN_BLOCK + si[None]
    mask = (dist >= 0) & (dist <= LOCAL_KEYS) & (kpos >= 0)
    o, lse = _masked_attend(s, vb, mask[None, :, None, None], 'bnkgqs,bnskd->bnkgqd')
    o = o.transpose(0, 1, 4, 2, 3, 5).reshape(b, nb * ATTN_BLOCK, KV_HEADS, Q_GROUP, HEAD_DIM)[:, :L]
    lse = lse.transpose(0, 1, 4, 2, 3).reshape(b, nb * ATTN_BLOCK, KV_HEADS, Q_GROUP)[:, :L]
    return o, lse


def _dilated_prompt(q, k, v):
    b, L = q.shape[:2]
    outs, lses = [], []
    for _, dil in DILATED_GROUPS:
        def split(t):
            t = t.reshape((b, L // dil, dil) + t.shape[2:])
            return jnp.swapaxes(t, 1, 2).reshape((b * dil, L // dil) + t.shape[3:])

        def merge(t):
            t = t.reshape((b, dil, L // dil) + t.shape[2:])
            t = jnp.swapaxes(t, 1, 2)
            return t.reshape((b, L) + t.shape[3:])

        o, lse = _band_attend(split(q), split(k), split(v))
        outs.append(merge(o))
        lses.append(merge(lse))
    return _merge_by_denominator(outs, lses)


def _dilated_decode(q, k_new, v_new, cache_k, cache_v, pos):
    w_buf = cache_k.shape[1]
    ls = q.shape[1]
    k_all = jnp.concatenate([cache_k.astype(k_new.dtype), k_new], axis=1)
    v_all = jnp.concatenate([cache_v.astype(v_new.dtype), v_new], axis=1)
    outs, lses = [], []
    for window, dil in DILATED_GROUPS:
        offs = jnp.arange(0, window + 1, dil)
        idx = w_buf + jnp.arange(ls)[:, None] - offs[None, :]
        valid = (idx >= 0) & (pos[:, None] - offs[None, :] >= 0)
        idx = jnp.maximum(idx, 0)
        kg, vg = k_all[:, idx], v_all[:, idx]
        s = jnp.einsum('blkgd,blskd->blkgs', q, kg).astype(jnp.float32) * (HEAD_DIM ** -0.5)
        o, lse = _masked_attend(s, vg, valid[None, :, None, None, :], 'blkgs,blskd->blkgd')
        outs.append(o)
        lses.append(lse)
    return _merge_by_denominator(outs, lses)


def _causal_conv(xbc, prev, w, bias):
    full = jnp.concatenate([prev.astype(xbc.dtype), xbc], axis=1)
    L = xbc.shape[1]
    out = bias + sum(full[:, i:i + L] * w[i] for i in range(CONV_WIDTH))
    return jax.nn.silu(out), full[:, L:]


def _ssd(x, dt, a, b_in, c_in, h0):
    bt, L = x.shape[:2]
    hg = SSM_HEADS // SSM_GROUPS
    q = min(SSD_CHUNK, L)
    nc = -(-L // q)
    pad = nc * q - L

    def chunk(t):
        t = jnp.pad(t.astype(jnp.float32), [(0, 0), (0, pad)] + [(0, 0)] * (t.ndim - 2))
        return t.reshape((bt, nc, q) + t.shape[2:])

    xc = chunk(x).reshape(bt, nc, q, SSM_GROUPS, hg, SSM_HEAD_DIM)
    dtc = chunk(dt).reshape(bt, nc, q, SSM_GROUPS, hg)
    bc, cc = chunk(b_in), chunk(c_in)
    acs = jnp.cumsum(dtc * a.astype(jnp.float32).reshape(SSM_GROUPS, hg), axis=2)
    causal = jnp.tril(jnp.ones((q, q), dtype=bool))
    seg = acs[:, :, :, None] - acs[:, :, None, :]
    decay = jnp.exp(jnp.where(causal[:, :, None, None], seg, -jnp.inf))
    xdt = xc * dtc[..., None]
    cb = jnp.einsum('bclgn,bcsgn->bclsg', cc, bc)
    y_diag = jnp.einsum('bclsg,bclsgh,bcsghp->bclghp', cb, decay, xdt)
    to_end = jnp.exp(acs[:, :, -1:] - acs)
    states = jnp.einsum('bcsgn,bcsgh,bcsghp->bcghpn', bc, to_end, xdt)
    chunk_decay = jnp.exp(acs[:, :, -1])

    def step(h, inp):
        st, dec = inp
        return dec[..., None, None] * h + st, h

    h0r = h0.astype(jnp.float32).reshape(bt, SSM_GROUPS, hg, SSM_HEAD_DIM, D_STATE)
    h_fin, h_prev = lax.scan(step, h0r, (jnp.moveaxis(states, 1, 0), jnp.moveaxis(chunk_decay, 1, 0)))
    h_prev = jnp.moveaxis(h_prev, 0, 1)
    y_off = jnp.einsum('bclgn,bcghpn,bclgh->bclghp', cc, h_prev, jnp.exp(acs))
    y = (y_diag + y_off).reshape(bt, nc * q, SSM_HEADS, SSM_HEAD_DIM)[:, :L]
    return y, h_fin.reshape(bt, SSM_HEADS, SSM_HEAD_DIM, D_STATE)


def _project(u, pos, w_in, q_norm_w, k_norm_w):
    b, L = u.shape[:2]
    proj = jnp.einsum('bld,de->ble', u, w_in)
    sizes = [ATTN_HEADS * HEAD_DIM, KV_HEADS * HEAD_DIM, KV_HEADS * HEAD_DIM, D_SSM, CONV_DIM]
    cuts = np.cumsum(sizes).tolist()
    q, k, v, z, xbc, dt_raw = jnp.split(proj, cuts, axis=-1)
    q = _rope(_rmsnorm(q.reshape(b, L, ATTN_HEADS, HEAD_DIM), q_norm_w), pos)
    k = _rope(_rmsnorm(k.reshape(b, L, KV_HEADS, HEAD_DIM), k_norm_w), pos)
    v = v.reshape(b, L, KV_HEADS, HEAD_DIM)
    return q.reshape(b, L, KV_HEADS, Q_GROUP, HEAD_DIM), k, v, z, xbc, dt_raw


def _ssd_branch(xbc, z, dt_raw, conv_prev, h0, conv_w, conv_b, dt_bias, a_log, d_skip, ssm_norm_w):
    b, L = xbc.shape[:2]
    xbc_c, conv_new = _causal_conv(xbc, conv_prev, conv_w, conv_b)
    xs, bs, cs = jnp.split(xbc_c, [D_SSM, D_SSM + SSM_GROUPS * D_STATE], axis=-1)
    xs = xs.reshape(b, L, SSM_HEADS, SSM_HEAD_DIM)
    bs = bs.reshape(b, L, SSM_GROUPS, D_STATE)
    cs = cs.reshape(b, L, SSM_GROUPS, D_STATE)
    dt = jax.nn.softplus(dt_raw.astype(jnp.float32) + dt_bias.astype(jnp.float32))
    a = -jnp.exp(a_log.astype(jnp.float32))
    y, h_new = _ssd(xs, dt, a, bs, cs, h0)
    y = y + d_skip.astype(jnp.float32)[:, None] * xs.astype(jnp.float32)
    yg = (y.reshape(b, L, D_SSM) * jax.nn.silu(z.astype(jnp.float32))).reshape(b, L, SSM_GROUPS, D_SSM // SSM_GROUPS)
    yg = yg * lax.rsqrt(jnp.mean(yg * yg, axis=-1, keepdims=True) + EPS)
    y = (yg.reshape(b, L, D_SSM) * ssm_norm_w.astype(jnp.float32)).astype(xbc.dtype)
    return y, conv_new, h_new.astype(h0.dtype)


def _peer(u, w_query, sub_keys, expert_u, expert_v):
    b, L, d = u.shape
    t = b * L
    nb = -(-t // PEER_BLOCK)
    pad = nb * PEER_BLOCK - t
    ub = jnp.pad(u.reshape(t, d), [(0, pad), (0, 0)]).reshape(nb, PEER_BLOCK, d)
    half = D_KEY // 2

    def block(xb):
        qh = jnp.einsum('td,de->te', xb, w_query).astype(jnp.float32).reshape(PEER_BLOCK, PEER_HEADS, D_KEY)
        s1 = jnp.einsum('thk,hnk->thn', qh[..., :half], sub_keys[:, 0].astype(jnp.float32))
        s2 = jnp.einsum('thk,hnk->thn', qh[..., half:], sub_keys[:, 1].astype(jnp.float32))
        v1, i1 = lax.top_k(s1, PEER_TOPK)
        v2, i2 = lax.top_k(s2, PEER_TOPK)
        cand = (v1[..., :, None] + v2[..., None, :]).reshape(PEER_BLOCK, PEER_HEADS, PEER_TOPK * PEER_TOPK)
        cidx = (i1[..., :, None] * N_KEYS + i2[..., None, :]).reshape(PEER_BLOCK, PEER_HEADS, PEER_TOPK * PEER_TOPK)
        top, sel = lax.top_k(cand, PEER_TOPK)
        eidx = jnp.take_along_axis(cidx, sel, axis=-1)
        gate = jax.nn.softmax(top, axis=-1)
        act = jax.nn.gelu(jnp.einsum('thkd,td->thk', expert_u[eidx], xb).astype(jnp.float32), approximate=False)
        return jnp.einsum('thk,thkd->td', (gate * act).astype(xb.dtype), expert_v[eidx])

    out = lax.map(block, ub)
    return out.reshape(nb * PEER_BLOCK, d)[:t].reshape(b, L, d)


def _layer(x, c, pos, attend, conv_prev, h0, w_ada, b_ada, norm_mix_w, norm_ffn_w, w_in, q_norm_w,
           k_norm_w, conv_w, conv_b, dt_bias, a_log, d_skip, ssm_norm_w, w_out, w_query, sub_keys,
           expert_u, expert_v):
    b, L = x.shape[:2]
    mod = jnp.einsum('bd,de->be', jax.nn.silu(c), w_ada) + b_ada
    sh1, sc1, g1, sh2, sc2, g2 = [m[:, None, :] for m in jnp.split(mod, 6, axis=-1)]
    u = _rmsnorm(x, norm_mix_w) * (1 + sc1) + sh1
    q, k, v, z, xbc, dt_raw = _project(u, pos, w_in, q_norm_w, k_norm_w)
    o_attn = attend(q, k, v)
    y_ssd, conv_new, h_new = _ssd_branch(xbc, z, dt_raw, conv_prev, h0, conv_w, conv_b, dt_bias,
                                         a_log, d_skip, ssm_norm_w)
    mixed = jnp.concatenate([o_attn.reshape(b, L, ATTN_HEADS * HEAD_DIM).astype(x.dtype), y_ssd], axis=-1)
    h = x + g1 * jnp.einsum('ble,ed->bld', mixed, w_out)
    un = _rmsnorm(h, norm_ffn_w) * (1 + sc2) + sh2
    y = h + g2 * _peer(un, w_query, sub_keys, expert_u, expert_v)
    return y, k, v, conv_new, h_new


def setup_inputs(seed: int = 0) -> dict:
    key = jax.random.key(seed)
    ks = jax.random.split(key, 26)
    f32 = jnp.float32

    def nrm(k, shape, scale=1.0):
        return scale * jax.random.normal(k, shape, f32)

    w_buf = min(MAX_WINDOW, PAST_LEN)
    dt0 = jnp.exp(jax.random.uniform(ks[18], (DEPTH, SSM_HEADS), f32, math.log(1e-3), math.log(1e-1)))
    return {
        'x_prompt': nrm(ks[0], (BATCH, SEQ, D_MODEL)),
        'x_sample': nrm(ks[1], (DEC_BATCH, DEC_SEQ, D_MODEL)),
        'cache_k': nrm(ks[2], (DEPTH, DEC_BATCH, w_buf, KV_HEADS, HEAD_DIM)),
        'cache_v': nrm(ks[3], (DEPTH, DEC_BATCH, w_buf, KV_HEADS, HEAD_DIM)),
        'state_ssm': nrm(ks[4], (DEPTH, DEC_BATCH, SSM_HEADS, SSM_HEAD_DIM, D_STATE), 0.1),
        'state_conv': nrm(ks[5], (DEPTH, DEC_BATCH, CONV_WIDTH - 1, CONV_DIM)),
        'c_prompt': nrm(ks[6], (BATCH, D_MODEL)),
        'c_sample': nrm(ks[7], (DEC_BATCH, D_MODEL)),
        'w_ada': nrm(ks[8], (DEPTH, D_MODEL, 6 * D_MODEL), 0.5 * D_MODEL ** -0.5),
        'b_ada': nrm(ks[9], (DEPTH, 6 * D_MODEL), 0.02),
        'norm_mix_w': 1.0 + nrm(ks[10], (DEPTH, D_MODEL), 0.02),
        'norm_ffn_w': 1.0 + nrm(ks[11], (DEPTH, D_MODEL), 0.02),
        'w_in': nrm(ks[12], (DEPTH, D_MODEL, D_IN_PROJ), D_MODEL ** -0.5),
        'q_norm_w': 1.0 + nrm(ks[13], (DEPTH, HEAD_DIM), 0.02),
        'k_norm_w': 1.0 + nrm(ks[14], (DEPTH, HEAD_DIM), 0.02),
        'conv_w': nrm(ks[15], (DEPTH, CONV_WIDTH, CONV_DIM), 0.5),
        'conv_b': nrm(ks[16], (DEPTH, CONV_DIM), 0.02),
        'dt_bias': dt0 + jnp.log(-jnp.expm1(-dt0)),
        'a_log': jnp.log(jax.random.uniform(ks[17], (DEPTH, SSM_HEADS), f32, 1.0, 16.0)),
        'd_skip': 1.0 + nrm(ks[19], (DEPTH, SSM_HEADS), 0.02),
        'ssm_norm_w': 1.0 + nrm(ks[20], (DEPTH, D_SSM), 0.02),
        'w_out': nrm(ks[21], (DEPTH, D_MIX, D_MODEL), D_MIX ** -0.5),
        'w_query': nrm(ks[22], (DEPTH, D_MODEL, PEER_HEADS * D_KEY), D_MODEL ** -0.5),
        'sub_keys': nrm(ks[23], (DEPTH, PEER_HEADS, 2, N_KEYS, D_KEY // 2), (D_KEY // 2) ** -0.5),
        'expert_u': nrm(ks[24], (DEPTH, N_EXPERTS, D_MODEL), D_MODEL ** -0.5),
        'expert_v': nrm(ks[25], (DEPTH, N_EXPERTS, D_MODEL), 0.5),
    }


def reference(x_prompt, x_sample, cache_k, cache_v, state_ssm, state_conv, c_prompt, c_sample,
              w_ada, b_ada, norm_mix_w, norm_ffn_w, w_in, q_norm_w, k_norm_w, conv_w, conv_b,
              dt_bias, a_log, d_skip, ssm_norm_w, w_out, w_query, sub_keys, expert_u, expert_v):
    bp, seq = x_prompt.shape[:2]
    ls = x_sample.shape[1]
    pos_p = jnp.arange(seq, dtype=jnp.int32)
    pos_s = PAST_LEN + jnp.arange(ls, dtype=jnp.int32)
    tail = min(MAX_WINDOW, seq)
    zeros_conv = jnp.zeros((bp, CONV_WIDTH - 1, CONV_DIM), x_prompt.dtype)
    zeros_h = jnp.zeros((bp, SSM_HEADS, SSM_HEAD_DIM, D_STATE), state_ssm.dtype)
    y_p, y_s = x_prompt, x_sample
    kp, vp, ksm, vsm, hp, hsm, cp, csm = [], [], [], [], [], [], [], []
    for l in range(DEPTH):
        lw = (w_ada[l], b_ada[l], norm_mix_w[l], norm_ffn_w[l], w_in[l], q_norm_w[l], k_norm_w[l],
              conv_w[l], conv_b[l], dt_bias[l], a_log[l], d_skip[l], ssm_norm_w[l], w_out[l],
              w_query[l], sub_keys[l], expert_u[l], expert_v[l])
        y_p, k_p, v_p, conv_p, h_p = _layer(y_p, c_prompt, pos_p, _dilated_prompt, zeros_conv, zeros_h, *lw)
        attend_s = functools.partial(_dilated_decode, cache_k=cache_k[l], cache_v=cache_v[l], pos=pos_s)
        y_s, k_s, v_s, conv_s, h_s = _layer(y_s, c_sample, pos_s, attend_s, state_conv[l], state_ssm[l], *lw)
        kp.append(k_p[:, seq - tail:])
        vp.append(v_p[:, seq - tail:])
        ksm.append(k_s)
        vsm.append(v_s)
        hp.append(h_p)
        hsm.append(h_s)
        cp.append(conv_p)
        csm.append(conv_s)
    k_prompt, v_prompt = jnp.stack(kp), jnp.stack(vp)
    k_sample, v_sample = jnp.stack(ksm), jnp.stack(vsm)
    ssm_prompt, ssm_sample = jnp.stack(hp), jnp.stack(hsm)
    conv_prompt, conv_sample = jnp.stack(cp), jnp.stack(csm)
    return (y_p, y_s, k_prompt, v_prompt, k_sample, v_sample, ssm_prompt, ssm_sample, conv_prompt, conv_sample)
```

```python
import functools
import math

import jax
import jax.numpy as jnp
from jax import lax
from jax.experimental import pallas as pl
from jax.experimental.pallas import tpu as pltpu

f32 = jnp.float32
i32 = jnp.int32
MXU_DTYPE = jnp.bfloat16

D_MODEL = 4096
ATTN_HEADS = 16
KV_HEADS = 4
HEAD_DIM = 128
Q_GROUP = ATTN_HEADS // KV_HEADS
D_ATTN = ATTN_HEADS * HEAD_DIM
D_KV = KV_HEADS * HEAD_DIM
DILATED_GROUPS = ((128, 1), (512, 4), (2048, 16))
MAX_WINDOW = 2048
PAST_LEN = 2048
ROPE_THETA = 10000.0
SSM_HEADS = 32
SSM_HEAD_DIM = 64
D_SSM = SSM_HEADS * SSM_HEAD_DIM
SSM_GROUPS = 4
HEADS_PER_GROUP = SSM_HEADS // SSM_GROUPS
D_GROUP = D_SSM // SSM_GROUPS
D_STATE = 128
CONV_WIDTH = 4
CONV_DIM = D_SSM + 2 * SSM_GROUPS * D_STATE
PEER_HEADS = 8
N_KEYS = 128
D_KEY = 256
PEER_TOPK = 16
N_PICK = PEER_HEADS * PEER_TOPK
EPS = 1e-6

LANES = 128
ROWS = 128
NORM_CHUNK = 32
VMEM_LIMIT = 56 * 1024 * 1024

OFF_Q, OFF_Z, OFF_K, OFF_V, OFF_XBC = 0, 2048, 4096, 4608, 5120
D_PROJ = OFF_XBC + CONV_DIM


def _mx(x):
    return x.astype(MXU_DTYPE)


def _dot(a, b):
    return jnp.dot(_mx(a), _mx(b), preferred_element_type=f32)


def _dot_nt(a, b):
    return lax.dot_general(_mx(a), _mx(b), (((1,), (1,)), ((), ())), preferred_element_type=f32)


def _params(*sem):
    return pltpu.CompilerParams(dimension_semantics=sem, vmem_limit_bytes=VMEM_LIMIT)


def _ada_kernel(c_ref, w_ref, b_ref, o_ref):
    c = c_ref[...]
    o_ref[...] = _dot(jax.nn.silu(c), w_ref[...]) + b_ref[...]


def _ada(c, w, b):
    m, d = c.shape
    n = w.shape[1]
    tn = 512
    return pl.pallas_call(
        _ada_kernel,
        grid=(n // tn,),
        in_specs=[pl.BlockSpec((m, d), lambda j: (0, 0)),
                  pl.BlockSpec((d, tn), lambda j: (0, j)),
                  pl.BlockSpec((1, tn), lambda j: (0, j))],
        out_specs=pl.BlockSpec((m, tn), lambda j: (0, j)),
        out_shape=jax.ShapeDtypeStruct((m, n), f32),
        compiler_params=_params("arbitrary"),
        name="ada",
    )(c, w, b.reshape(1, n))


def _nmm_kernel(emit_u, has_extra, *refs):
    x_ref, nw_ref, sc_ref, sh_ref, w_ref = refs[:5]
    refs = refs[5:]
    if has_extra:
        wx_ref, refs = refs[0], refs[1:]
    o_ref, refs = refs[0], refs[1:]
    if has_extra:
        ox_ref, refs = refs[0], refs[1:]
    if emit_u:
        u_ref, refs = refs[0], refs[1:]
    u_scr = refs[0]

    @pl.when(pl.program_id(1) == 0)
    def _():
        def rows(c, _):
            sl = pl.ds(pl.multiple_of(c * NORM_CHUNK, NORM_CHUNK), NORM_CHUNK)
            x = x_ref[sl, :]
            y = x * lax.rsqrt(jnp.mean(x * x, axis=-1, keepdims=True) + EPS) * nw_ref[...]
            sc = sc_ref[...] if sc_ref.shape[0] == 1 else sc_ref[sl, :]
            sh = sh_ref[...] if sh_ref.shape[0] == 1 else sh_ref[sl, :]
            u = y * (1.0 + sc) + sh
            u_scr[sl, :] = _mx(u)
            if emit_u:
                u_ref[sl, :] = u
            return 0
        lax.fori_loop(0, x_ref.shape[0] // NORM_CHUNK, rows, 0)
        if has_extra:
            ox_ref[...] = jnp.dot(u_scr[...], wx_ref[...], preferred_element_type=f32)

    o_ref[...] = jnp.dot(u_scr[...], w_ref[...], preferred_element_type=f32)


def _norm_mod_matmul(x, nw, sc, sh, w, w_extra=None, emit_u=False, tn=512):
    t, d = x.shape
    n = w.shape[1]
    mod_rows = sc.shape[0]
    tm = min(512 if mod_rows == 1 else 256, t)
    mod_spec = (pl.BlockSpec((1, d), lambda i, j: (0, 0)) if mod_rows == 1
                else pl.BlockSpec((tm, d), lambda i, j: (i, 0)))
    in_specs = [pl.BlockSpec((tm, d), lambda i, j: (i, 0)),
                pl.BlockSpec((1, d), lambda i, j: (0, 0)),
                mod_spec, mod_spec,
                pl.BlockSpec((d, tn), lambda i, j: (0, j))]
    args = [x, nw.reshape(1, d), sc, sh, w]
    out_specs = [pl.BlockSpec((tm, tn), lambda i, j: (i, j))]
    out_shape = [jax.ShapeDtypeStruct((t, n), f32)]
    if w_extra is not None:
        nx = w_extra.shape[1]
        in_specs.append(pl.BlockSpec((d, nx), lambda i, j: (0, 0)))
        args.append(w_extra)
        out_specs.append(pl.BlockSpec((tm, nx), lambda i, j: (i, 0)))
        out_shape.append(jax.ShapeDtypeStruct((t, nx), f32))
    if emit_u:
        out_specs.append(pl.BlockSpec((tm, d), lambda i, j: (i, 0)))
        out_shape.append(jax.ShapeDtypeStruct((t, d), f32))
    return pl.pallas_call(
        functools.partial(_nmm_kernel, emit_u, w_extra is not None),
        grid=(t // tm, n // tn),
        in_specs=in_specs, out_specs=out_specs, out_shape=out_shape,
        scratch_shapes=[pltpu.VMEM((tm, d), MXU_DTYPE)],
        compiler_params=_params("arbitrary", "arbitrary"),
        name="norm_mod_matmul",
    )(*args)


def _qkrope_kernel(q_ref, k_ref, cos_ref, sin_ref, qw_ref, kw_ref, qo_ref, ko_ref):
    cos = cos_ref[...]
    sin = sin_ref[...]

    def norm_rope(x, w):
        y = x * lax.rsqrt(jnp.mean(x * x, axis=-1, keepdims=True) + EPS) * w
        return y * cos + pltpu.roll(y, HEAD_DIM // 2, 1) * sin

    for h in range(ATTN_HEADS):
        sl = slice(h * HEAD_DIM, (h + 1) * HEAD_DIM)
        qo_ref[:, sl] = norm_rope(q_ref[:, sl], qw_ref[...]).astype(qo_ref.dtype)
    for h in range(KV_HEADS):
        sl = slice(h * HEAD_DIM, (h + 1) * HEAD_DIM)
        ko_ref[:, sl] = norm_rope(k_ref[:, sl], kw_ref[...])


def _qk_rope(proj, cos, sin, qw, kw, q_dtype, tm=256):
    t = proj.shape[0]
    tm = min(tm, t)
    return pl.pallas_call(
        _qkrope_kernel,
        grid=(t // tm,),
        in_specs=[pl.BlockSpec((tm, D_ATTN), lambda i: (i, OFF_Q // D_ATTN)),
                  pl.BlockSpec((tm, D_KV), lambda i: (i, OFF_K // D_KV)),
                  pl.BlockSpec((tm, HEAD_DIM), lambda i: (i, 0)),
                  pl.BlockSpec((tm, HEAD_DIM), lambda i: (i, 0)),
                  pl.BlockSpec((1, HEAD_DIM), lambda i: (0, 0)),
                  pl.BlockSpec((1, HEAD_DIM), lambda i: (0, 0))],
        out_specs=[pl.BlockSpec((tm, D_ATTN), lambda i: (i, 0)),
                   pl.BlockSpec((tm, D_KV), lambda i: (i, 0))],
        out_shape=[jax.ShapeDtypeStruct((t, D_ATTN), q_dtype),
                   jax.ShapeDtypeStruct((t, D_KV), f32)],
        compiler_params=_params("arbitrary"),
        name="qk_rope",
    )(proj, proj, cos, sin, qw.reshape(1, HEAD_DIM), kw.reshape(1, HEAD_DIM))


def _rope_tables(pos):
    half = HEAD_DIM // 2
    inv = ROPE_THETA ** (-jnp.arange(half, dtype=f32) / half)
    ang = pos.astype(f32)[:, None] * inv[None, :]
    cos, sin = jnp.cos(ang), jnp.sin(ang)
    return jnp.concatenate([cos, cos], axis=-1), jnp.concatenate([-sin, sin], axis=-1)


def _multiplicity(dist):
    cnt = jnp.zeros(dist.shape, f32)
    for window, dil in DILATED_GROUPS:
        hit = (dist >= 0) & (dist <= window) & ((dist & (dil - 1)) == 0)
        cnt = cnt + jnp.where(hit, 1.0, 0.0)
    return cnt


def _attn_prompt_kernel(span, q_ref, k_ref, v_ref, o_ref):
    n = pl.program_id(1)
    start = pl.multiple_of(jnp.maximum(n - MAX_WINDOW // ROWS, 0) * ROWS, ROWS)
    ks = k_ref[pl.ds(start, span), :]
    vs = v_ref[pl.ds(start, span), :]
    q4 = jnp.concatenate([q_ref[:, h * HEAD_DIM:(h + 1) * HEAD_DIM] for h in range(Q_GROUP)], axis=0)
    s = _dot_nt(q4, ks) * (HEAD_DIM ** -0.5)
    qpos = n * ROWS + lax.broadcasted_iota(i32, (ROWS, span), 0)
    kpos = start + lax.broadcasted_iota(i32, (ROWS, span), 1)
    cnt = _multiplicity(qpos - kpos)[None]
    s = jnp.where(cnt > 0, s.reshape(Q_GROUP, ROWS, span), -jnp.inf)
    m = jnp.max(s, axis=-1, keepdims=True)
    p = cnt * jnp.exp(s - m)
    l = jnp.sum(p, axis=-1, keepdims=True)
    o = _dot(p.reshape(Q_GROUP * ROWS, span), vs).reshape(Q_GROUP, ROWS, HEAD_DIM) / l
    for h in range(Q_GROUP):
        o_ref[:, h * HEAD_DIM:(h + 1) * HEAD_DIM] = o[h].astype(o_ref.dtype)


def _attn_prompt(q, k, proj):
    L = q.shape[0]
    span = min(L, MAX_WINDOW + ROWS)
    gw = Q_GROUP * HEAD_DIM
    return pl.pallas_call(
        functools.partial(_attn_prompt_kernel, span),
        grid=(KV_HEADS, L // ROWS),
        in_specs=[pl.BlockSpec((ROWS, gw), lambda g, n: (n, g)),
                  pl.BlockSpec((L, HEAD_DIM), lambda g, n: (0, g)),
                  pl.BlockSpec((L, HEAD_DIM), lambda g, n: (0, OFF_V // HEAD_DIM + g))],
        out_specs=pl.BlockSpec((ROWS, gw), lambda g, n: (n, g)),
        out_shape=jax.ShapeDtypeStruct((L, D_ATTN), MXU_DTYPE),
        compiler_params=_params("arbitrary", "arbitrary"),
        name="attn_prompt",
    )(q, k, proj)


def _attn_sample_kernel(ls, q_ref, kn_ref, vn_ref, ck_ref, cv_ref, o_ref):
    w_buf = ck_ref.shape[1]
    rows = Q_GROUP * ls
    t_row = lax.broadcasted_iota(i32, (rows, 1), 0) & ((1 << _log2(ls)) - 1)
    cnt_c = _multiplicity(w_buf + t_row - lax.broadcasted_iota(i32, (rows, w_buf), 1))
    col_n = lax.broadcasted_iota(i32, (rows, LANES), 1)
    cnt_n = jnp.where(col_n < ls, _multiplicity(t_row - col_n), 0.0)
    pad = jnp.zeros((LANES - ls, HEAD_DIM), f32)
    for g in range(KV_HEADS):
        sl = slice(g * HEAD_DIM, (g + 1) * HEAD_DIM)
        qg = jnp.concatenate(
            [q_ref[:, (g * Q_GROUP + h) * HEAD_DIM:(g * Q_GROUP + h + 1) * HEAD_DIM] for h in range(Q_GROUP)], axis=0)
        kn = jnp.concatenate([kn_ref[:, sl], pad], axis=0)
        vn = jnp.concatenate([vn_ref[:, sl], pad], axis=0)
        s_c = jnp.where(cnt_c > 0, _dot_nt(qg, ck_ref[0, :, sl]) * (HEAD_DIM ** -0.5), -jnp.inf)
        s_n = jnp.where(cnt_n > 0, _dot_nt(qg, kn) * (HEAD_DIM ** -0.5), -jnp.inf)
        m = jnp.maximum(jnp.max(s_c, axis=-1, keepdims=True), jnp.max(s_n, axis=-1, keepdims=True))
        p_c = cnt_c * jnp.exp(s_c - m)
        p_n = cnt_n * jnp.exp(s_n - m)
        l = jnp.sum(p_c, axis=-1, keepdims=True) + jnp.sum(p_n, axis=-1, keepdims=True)
        o = (_dot(p_c, cv_ref[0, :, sl]) + _dot(p_n, vn)) / l
        for h in range(Q_GROUP):
            c0 = (g * Q_GROUP + h) * HEAD_DIM
            o_ref[:, c0:c0 + HEAD_DIM] = o[h * ls:(h + 1) * ls]


def _attn_sample(q, k, proj, cache_k, cache_v, ls):
    t = q.shape[0]
    b, w_buf, _ = cache_k.shape
    return pl.pallas_call(
        functools.partial(_attn_sample_kernel, ls),
        grid=(b,),
        in_specs=[pl.BlockSpec((ls, D_ATTN), lambda i: (i, 0)),
                  pl.BlockSpec((ls, D_KV), lambda i: (i, 0)),
                  pl.BlockSpec((ls, D_KV), lambda i: (i, OFF_V // D_KV)),
                  pl.BlockSpec((1, w_buf, D_KV), lambda i: (i, 0, 0)),
                  pl.BlockSpec((1, w_buf, D_KV), lambda i: (i, 0, 0))],
        out_specs=pl.BlockSpec((ls, D_ATTN), lambda i: (i, 0)),
        out_shape=jax.ShapeDtypeStruct((t, D_ATTN), f32),
        compiler_params=_params("arbitrary"),
        name="attn_sample",
    )(q, k, proj, cache_k, cache_v)


def _log2(n):
    assert n > 0 and n & (n - 1) == 0, n
    return n.bit_length() - 1


def _expand_matrix(transposed):
    shape = (D_GROUP, LANES) if transposed else (LANES, D_GROUP)
    head = lax.broadcasted_iota(i32, shape, 1 if transposed else 0)
    chan = lax.broadcasted_iota(i32, shape, 0 if transposed else 1)
    return jnp.where((chan >> _log2(SSM_HEAD_DIM)) == head, 1.0, 0.0).astype(jnp.bfloat16)


def _split3(v):
    hi = v.astype(jnp.bfloat16)
    r1 = v - hi.astype(f32)
    mid = r1.astype(jnp.bfloat16)
    lo = (r1 - mid.astype(f32)).astype(jnp.bfloat16)
    return hi, mid, lo


def _expand_lanes(v, e):
    return sum(jnp.dot(p, e, preferred_element_type=f32) for p in _split3(v))


def _expand_rows(v, et):
    return sum(jnp.dot(et, p, preferred_element_type=f32) for p in _split3(v))


def _ssd_kernel(seg, carry, x0_ref, x1_ref, x2_ref, x3_ref, b0_ref, b1_ref, b2_ref, b3_ref,
                c0_ref, c1_ref, c2_ref, c3_ref, z_ref, dt_ref, cwx_ref, cwb_ref, cwc_ref,
                cbx_ref, cbb_ref, cbc_ref, dtb_ref, alog_ref, dskip_ref, nw_ref, *rest):
    if carry:
        y_ref, hout_ref = rest
    else:
        h0_ref, y_ref, hout_ref = rest
    nseg = ROWS // seg

    def conv(taps, w_ref, b_ref):
        acc = b_ref[...] + taps[0][...] * w_ref[0:1, :]
        for i in range(1, CONV_WIDTH):
            acc = acc + taps[i][...] * w_ref[i:i + 1, :]
        return jax.nn.silu(acc)

    xc = conv((x0_ref, x1_ref, x2_ref, x3_ref), cwx_ref, cbx_ref)
    bc = conv((b0_ref, b1_ref, b2_ref, b3_ref), cwb_ref, cbb_ref)
    cc = conv((c0_ref, c1_ref, c2_ref, c3_ref), cwc_ref, cbc_ref)

    dt = jax.nn.softplus(dt_ref[...] + dtb_ref[...])
    acs = dt * (-jnp.exp(alog_ref[...]))
    lseg = _log2(seg)
    row = lax.broadcasted_iota(i32, (ROWS, LANES), 0) & (seg - 1)
    sh = 1
    while sh < seg:
        acs = acs + jnp.where(row >= sh, pltpu.roll(acs, sh, 0), 0.0)
        sh *= 2

    e = _expand_matrix(False)
    dt_e = _expand_lanes(dt, e)
    acs_e = _expand_lanes(acs, e)
    acs_t = acs.T
    acs_te = _expand_rows(acs_t, _expand_matrix(True))

    xdt = xc * dt_e
    acs_e3 = acs_e.reshape(nseg, seg, D_GROUP)
    to_end = jnp.exp(acs_e3[:, seg - 1:seg, :] - acs_e3).reshape(ROWS, D_GROUP)
    xdt_te_t = (xdt * to_end).T

    cb = _dot_nt(cc, bc)
    li = lax.broadcasted_iota(i32, (ROWS, ROWS), 0)
    si = lax.broadcasted_iota(i32, (ROWS, ROWS), 1)
    causal = (li >= si) & ((li >> lseg) == (si >> lseg))
    lane_head = lax.broadcasted_iota(i32, (ROWS, D_GROUP), 1) >> _log2(SSM_HEAD_DIM)
    y = dskip_ref[...] * xc
    for j in range(HEADS_PER_GROUP):
        decay = jnp.exp(jnp.where(causal, acs[:, j:j + 1] - acs_t[j:j + 1, :], -jnp.inf))
        y = y + _dot(cb * decay, jnp.where(lane_head == j, xdt, 0.0))

    row_seg = lax.broadcasted_iota(i32, (ROWS, D_GROUP), 0) >> lseg
    lane_seg = lax.broadcasted_iota(i32, (D_GROUP, ROWS), 1) >> lseg
    if carry:
        @pl.when(pl.program_id(1) == 0)
        def _():
            hout_ref[...] = jnp.zeros(hout_ref.shape, f32)
    y_off = jnp.zeros((ROWS, D_GROUP), f32)
    for b in range(nseg):
        h_prev = (hout_ref[...] if carry else h0_ref[b]).reshape(D_GROUP, D_STATE)
        yo = _dot_nt(cc, h_prev)
        y_off = y_off + (yo if nseg == 1 else jnp.where(row_seg == b, yo, 0.0))
        te_b = xdt_te_t if nseg == 1 else jnp.where(lane_seg == b, xdt_te_t, 0.0)
        last = b * seg + seg - 1
        h_new = jnp.exp(acs_te[:, last:last + 1]) * h_prev + _dot(te_b, bc)
        h_new = h_new.reshape(HEADS_PER_GROUP, SSM_HEAD_DIM, D_STATE)
        if carry:
            hout_ref[...] = h_new
        else:
            hout_ref[b] = h_new
    y = y + y_off * jnp.exp(acs_e)

    yg = y * jax.nn.silu(z_ref[...])
    yg = yg * lax.rsqrt(jnp.mean(yg * yg, axis=-1, keepdims=True) + EPS)
    y_ref[...] = yg * nw_ref[...]


def _ssd(proj, shifted, dt_raw, seg, h0, cw, cb, dtb, alog, dskip, nw):
    t = proj.shape[0]
    carry = h0 is None
    nseg = ROWS // seg
    nblk = t // ROWS
    if carry:
        grid = (SSM_GROUPS, nblk)
        gi = lambda a, b: a
        ri = lambda a, b: b
    else:
        grid = (nblk, SSM_GROUPS)
        gi = lambda a, b: b
        ri = lambda a, b: a
    xb = OFF_XBC // D_GROUP
    bb = (OFF_XBC + D_SSM) // D_STATE
    cb_ = bb + SSM_GROUPS
    sbb = D_SSM // D_STATE

    def spec(width, fn):
        return pl.BlockSpec((ROWS, width), lambda a, b: (ri(a, b), fn(gi(a, b))))

    def pspec(rows, width, fn):
        return pl.BlockSpec((rows, width), lambda a, b: (0, fn(gi(a, b))))

    s3, s2, s1 = shifted
    in_specs = (
        [spec(D_GROUP, lambda g: g)] * 3 + [spec(D_GROUP, lambda g: xb + g)]
        + [spec(D_STATE, lambda g: sbb + g)] * 3 + [spec(D_STATE, lambda g: bb + g)]
        + [spec(D_STATE, lambda g: sbb + SSM_GROUPS + g)] * 3 + [spec(D_STATE, lambda g: cb_ + g)]
        + [spec(D_GROUP, lambda g: OFF_Z // D_GROUP + g), spec(LANES, lambda g: g)]
        + [pspec(CONV_WIDTH, D_GROUP, lambda g: g), pspec(CONV_WIDTH, D_STATE, lambda g: sbb + g),
           pspec(CONV_WIDTH, D_STATE, lambda g: sbb + SSM_GROUPS + g)]
        + [pspec(1, D_GROUP, lambda g: g), pspec(1, D_STATE, lambda g: sbb + g),
           pspec(1, D_STATE, lambda g: sbb + SSM_GROUPS + g)]
        + [pspec(1, LANES, lambda g: g)] * 2 + [pspec(1, D_GROUP, lambda g: g)] * 2)
    args = [s3, s2, s1, proj, s3, s2, s1, proj, s3, s2, s1, proj, proj, dt_raw,
            cw, cw, cw, cb, cb, cb, dtb, alog, dskip, nw]
    hblk = (HEADS_PER_GROUP, SSM_HEAD_DIM, D_STATE)
    if carry:
        h_spec = pl.BlockSpec(hblk, lambda a, b: (a, 0, 0))
        h_shape = jax.ShapeDtypeStruct((SSM_HEADS, SSM_HEAD_DIM, D_STATE), f32)
    else:
        h_spec = pl.BlockSpec((nseg,) + hblk, lambda a, b: (a, b, 0, 0))
        h_shape = jax.ShapeDtypeStruct(h0.shape, f32)
        in_specs.append(h_spec)
        args.append(h0)
    return pl.pallas_call(
        functools.partial(_ssd_kernel, seg, carry),
        grid=grid,
        in_specs=in_specs,
        out_specs=[spec(D_GROUP, lambda g: g), h_spec],
        out_shape=[jax.ShapeDtypeStruct((t, D_SSM), f32), h_shape],
        compiler_params=_params("arbitrary", "arbitrary"),
        name="ssd",
    )(*args)


def _outproj_kernel(a1_ref, a2_ref, w1_ref, w2_ref, x_ref, g_ref, o_ref):
    acc = _dot(a1_ref[...], w1_ref[...]) + _dot(a2_ref[...], w2_ref[...])
    o_ref[...] = x_ref[...] + g_ref[...] * acc


def _out_proj(a1, a2, w1, w2, x, g, tm=512, tn=1024):
    t, d = x.shape
    tm = min(tm, t)
    k1, k2 = a1.shape[1], a2.shape[1]
    g_spec = (pl.BlockSpec((1, tn), lambda i, j: (0, j)) if g.shape[0] == 1
              else pl.BlockSpec((tm, tn), lambda i, j: (i, j)))
    return pl.pallas_call(
        _outproj_kernel,
        grid=(t // tm, d // tn),
        in_specs=[pl.BlockSpec((tm, k1), lambda i, j: (i, 0)),
                  pl.BlockSpec((tm, k2), lambda i, j: (i, 0)),
                  pl.BlockSpec((k1, tn), lambda i, j: (0, j)),
                  pl.BlockSpec((k2, tn), lambda i, j: (0, j)),
                  pl.BlockSpec((tm, tn), lambda i, j: (i, j)),
                  g_spec],
        out_specs=pl.BlockSpec((tm, tn), lambda i, j: (i, j)),
        out_shape=jax.ShapeDtypeStruct((t, d), f32),
        compiler_params=_params("arbitrary", "arbitrary"),
        name="out_proj",
    )(a1, a2, w1, w2, x, g)


def _take_max(s, rowi, n_rows):
    m = jnp.max(s, axis=0, keepdims=True)
    j = jnp.min(jnp.where(s == m, rowi, n_rows), axis=0, keepdims=True)
    return m, j, jnp.where(rowi == j, -jnp.inf, s)


def _topk_kernel(qh_ref, k1_ref, k2_ref, e_ref, g_ref):
    tm = qh_ref.shape[0]
    half = D_KEY // 2
    n_cand = PEER_TOPK * PEER_TOPK
    s1 = _dot_nt(k1_ref[0], qh_ref[:, :half])
    s2 = _dot_nt(k2_ref[0], qh_ref[:, half:])
    rowk = lax.broadcasted_iota(i32, (N_KEYS, tm), 0)
    row16 = lax.broadcasted_iota(i32, (PEER_TOPK, tm), 0)

    def half_step(k, c):
        s1, s2, v1, i1, v2, i2 = c
        m1, j1, s1 = _take_max(s1, rowk, N_KEYS)
        m2, j2, s2 = _take_max(s2, rowk, N_KEYS)
        sel = row16 == k
        return (s1, s2, jnp.where(sel, m1, v1), jnp.where(sel, j1, i1),
                jnp.where(sel, m2, v2), jnp.where(sel, j2, i2))

    zf = jnp.zeros((PEER_TOPK, tm), f32)
    zi = jnp.zeros((PEER_TOPK, tm), i32)
    _, _, v1, i1, v2, i2 = lax.fori_loop(0, PEER_TOPK, half_step, (s1, s2, zf, zi, zf, zi))
    cand = jnp.concatenate([v1[a:a + 1] + v2 for a in range(PEER_TOPK)], axis=0)
    cidx = jnp.concatenate([i1[a:a + 1] * N_KEYS + i2 for a in range(PEER_TOPK)], axis=0)
    rowc = lax.broadcasted_iota(i32, (n_cand, tm), 0)

    def final_step(k, c):
        cand, top, eidx = c
        m, pos, cand = _take_max(cand, rowc, n_cand)
        e = jnp.max(jnp.where(rowc == pos, cidx, -1), axis=0, keepdims=True)
        sel = row16 == k
        return cand, jnp.where(sel, m, top), jnp.where(sel, e, eidx)

    _, top, eidx = lax.fori_loop(0, PEER_TOPK, final_step, (cand, zf, zi))
    ex = jnp.exp(top - top[0:1])
    e_ref[...] = eidx
    g_ref[...] = ex / jnp.sum(ex, axis=0, keepdims=True)


def _peer_topk(qh, keys1, keys2, tm=256):
    t = qh.shape[0]
    tm = min(tm, t)
    half = D_KEY // 2
    return pl.pallas_call(
        _topk_kernel,
        grid=(t // tm, PEER_HEADS),
        in_specs=[pl.BlockSpec((tm, D_KEY), lambda i, h: (i, h)),
                  pl.BlockSpec((1, N_KEYS, half), lambda i, h: (h, 0, 0)),
                  pl.BlockSpec((1, N_KEYS, half), lambda i, h: (h, 0, 0))],
        out_specs=[pl.BlockSpec((PEER_TOPK, tm), lambda i, h: (h, i)),
                   pl.BlockSpec((PEER_TOPK, tm), lambda i, h: (h, i))],
        out_shape=[jax.ShapeDtypeStruct((N_PICK, t), i32), jax.ShapeDtypeStruct((N_PICK, t), f32)],
        compiler_params=_params("arbitrary", "arbitrary"),
        name="peer_topk",
    )(qh, keys1, keys2)


GATHER_UNROLL = 8


def _peer_kernel(eidx_ref, gate_ref, un_ref, h_ref, g2_ref, u_hbm, v_hbm, y_ref, ubuf, vbuf, sem):
    tb = un_ref.shape[0]

    def row_copies(t, k, slot):
        e = eidx_ref[k, t]
        return (pltpu.make_async_copy(u_hbm.at[pl.ds(e, 1)], ubuf.at[slot, pl.ds(k, 1)], sem.at[0, slot]),
                pltpu.make_async_copy(v_hbm.at[pl.ds(e, 1)], vbuf.at[slot, pl.ds(k, 1)], sem.at[1, slot]))

    def issue(t, slot):
        def chunk(c, _):
            for i in range(GATHER_UNROLL):
                cu, cv = row_copies(t, c * GATHER_UNROLL + i, slot)
                cu.start()
                cv.start()
            return 0
        lax.fori_loop(0, N_PICK // GATHER_UNROLL, chunk, 0)

    def wait(slot):
        pltpu.make_async_copy(u_hbm.at[pl.ds(0, N_PICK)], ubuf.at[slot], sem.at[0, slot]).wait()
        pltpu.make_async_copy(v_hbm.at[pl.ds(0, N_PICK)], vbuf.at[slot], sem.at[1, slot]).wait()

    lane_t = lax.broadcasted_iota(i32, (N_PICK, tb), 1)

    def compute(t, slot):
        x = un_ref[pl.ds(t, 1), :]
        d = jnp.sum(ubuf[slot] * x, axis=-1, keepdims=True)
        act = 0.5 * d * (1.0 + lax.erf(d * (2.0 ** -0.5)))
        gate = jnp.sum(jnp.where(lane_t == t, gate_ref[...], 0.0), axis=-1, keepdims=True)
        o = jnp.sum((gate * act) * vbuf[slot], axis=0, keepdims=True)
        g2 = g2_ref[...] if g2_ref.shape[0] == 1 else g2_ref[pl.ds(t, 1), :]
        y_ref[pl.ds(t, 1), :] = h_ref[pl.ds(t, 1), :] + g2 * o

    issue(0, 0)

    def pair(i, _):
        t0 = 2 * i
        issue(t0 + 1, 1)
        wait(0)
        compute(t0, 0)

        @pl.when(t0 + 2 < tb)
        def _():
            issue(t0 + 2, 0)

        wait(1)
        compute(t0 + 1, 1)
        return 0

    lax.fori_loop(0, tb // 2, pair, 0)


def _peer(eidx, gate, un, h, g2, expert_u, expert_v):
    t, d = un.shape
    tb = ROWS
    g_spec = (pl.BlockSpec((1, d), lambda i: (0, 0)) if g2.shape[0] == 1
              else pl.BlockSpec((tb, d), lambda i: (i, 0)))
    return pl.pallas_call(
        _peer_kernel,
        grid=(t // tb,),
        in_specs=[pl.BlockSpec((N_PICK, tb), lambda i: (0, i), memory_space=pltpu.SMEM),
                  pl.BlockSpec((N_PICK, tb), lambda i: (0, i)),
                  pl.BlockSpec((tb, d), lambda i: (i, 0)),
                  pl.BlockSpec((tb, d), lambda i: (i, 0)),
                  g_spec,
                  pl.BlockSpec(memory_space=pl.ANY),
                  pl.BlockSpec(memory_space=pl.ANY)],
        out_specs=pl.BlockSpec((tb, d), lambda i: (i, 0)),
        out_shape=jax.ShapeDtypeStruct((t, d), f32),
        scratch_shapes=[pltpu.VMEM((2, N_PICK, d), f32), pltpu.VMEM((2, N_PICK, d), f32),
                        pltpu.SemaphoreType.DMA((2, 2))],
        compiler_params=_params("arbitrary"),
        name="peer_mix",
    )(eidx, gate, un, h, g2, expert_u, expert_v)


def _token_rows(m, n_tok):
    return m if m.shape[0] == 1 else jnp.repeat(m, n_tok, axis=0)


def _group_lanes(v):
    v = v.reshape(v.shape[:-1] + (SSM_GROUPS, HEADS_PER_GROUP))
    v = jnp.pad(v, [(0, 0)] * (v.ndim - 1) + [(0, LANES - HEADS_PER_GROUP)])
    return v.reshape(v.shape[:-2] + (SSM_GROUPS * LANES,))


def _shifted(xbc, prev):
    b, L, c = xbc.shape
    full = jnp.concatenate([prev, xbc], axis=1)
    return [full[:, i:i + L].reshape(b * L, c) for i in range(CONV_WIDTH - 1)]


def _layer(x, mods, pos, attend, conv_prev, h0, seg, p):
    b, L, d = x.shape
    t = b * L
    sh1, sc1, g1, sh2, sc2, g2 = [_token_rows(m, L) for m in mods]
    x2 = x.reshape(t, d)
    proj, dt_raw = _norm_mod_matmul(x2, p["norm_mix_w"], sc1, sh1, p["w_in"], w_extra=p["w_dt"])
    cos, sin = _rope_tables(jnp.tile(pos, b))
    q, k = _qk_rope(proj, cos, sin, p["q_norm_w"], p["k_norm_w"], attend.q_dtype)
    o_attn = attend(q, k, proj)
    xbc = proj[:, OFF_XBC:].reshape(b, L, CONV_DIM)
    y_ssd, h_new = _ssd(proj, _shifted(xbc, conv_prev), dt_raw, seg, h0, p["conv_w"], p["conv_b"],
                        p["dt_bias"], p["a_log"], p["d_skip"], p["ssm_norm_w"])
    conv_new = jnp.concatenate([conv_prev, xbc], axis=1)[:, L:]
    h = _out_proj(o_attn, y_ssd, p["w_out_a"], p["w_out_s"], x2, g1)
    qh, un = _norm_mod_matmul(h, p["norm_ffn_w"], sc2, sh2, p["w_query"], emit_u=True)
    eidx, gate = _peer_topk(qh, p["keys1"], p["keys2"])
    y = _peer(eidx, gate, un, h, g2, p["expert_u"], p["expert_v"])
    k = k.reshape(b, L, KV_HEADS, HEAD_DIM)
    v = proj[:, OFF_V:OFF_V + D_KV].reshape(b, L, KV_HEADS, HEAD_DIM)
    return y.reshape(b, L, d), k, v, conv_new, h_new


def _prep_weights(w_in, w_out, w_query, sub_keys, conv_b, dt_bias, a_log, d_skip, ssm_norm_w):
    cuts = [D_ATTN, D_ATTN + D_KV, D_ATTN + 2 * D_KV, D_ATTN + 2 * D_KV + D_SSM,
            D_ATTN + 2 * D_KV + D_SSM + CONV_DIM]
    wq, wk, wv, wz, wxbc, wdt = jnp.split(w_in, cuts, axis=1)
    return {
        "w_in": _mx(jnp.concatenate([wq, wz, wk, wv, wxbc], axis=1)),
        "w_dt": _mx(_group_lanes(wdt)),
        "w_out_a": _mx(w_out[:D_ATTN]), "w_out_s": _mx(w_out[D_ATTN:]),
        "w_query": _mx(w_query),
        "keys1": _mx(sub_keys[:, 0]), "keys2": _mx(sub_keys[:, 1]),
        "conv_b": conv_b.reshape(1, CONV_DIM),
        "dt_bias": _group_lanes(dt_bias.reshape(1, SSM_HEADS)),
        "a_log": _group_lanes(a_log.reshape(1, SSM_HEADS)),
        "d_skip": jnp.repeat(d_skip, SSM_HEAD_DIM).reshape(1, D_SSM),
        "ssm_norm_w": ssm_norm_w.reshape(1, D_SSM),
    }


def kernel(x_prompt, x_sample, cache_k, cache_v, state_ssm, state_conv, c_prompt, c_sample, w_ada, b_ada, norm_mix_w, norm_ffn_w, w_in, q_norm_w, k_norm_w, conv_w, conv_b, dt_bias, a_log, d_skip, ssm_norm_w, w_out, w_query, sub_keys, expert_u, expert_v):
    bp, seq = x_prompt.shape[:2]
    bs, ls = x_sample.shape[:2]
    depth = w_ada.shape[0]
    assert bp == 1 and seq % ROWS == 0 and ROWS % ls == 0 and (bs * ls) % ROWS == 0
    pos_p = jnp.arange(seq, dtype=i32)
    pos_s = PAST_LEN + jnp.arange(ls, dtype=i32)
    tail = min(MAX_WINDOW, seq)
    zeros_conv = jnp.zeros((bp, CONV_WIDTH - 1, CONV_DIM), f32)
    n_c = bp + bs
    c_all = jnp.pad(jnp.concatenate([c_prompt, c_sample], axis=0), [(0, -n_c % 8), (0, 0)])

    def attend_p(q, k, proj):
        return _attn_prompt(q, k, proj)
    attend_p.q_dtype = MXU_DTYPE

    y_p, y_s = x_prompt, x_sample
    outs = [[] for _ in range(8)]
    for l in range(depth):
        p = _prep_weights(w_in[l], w_out[l], w_query[l], sub_keys[l], conv_b[l], dt_bias[l], a_log[l],
                          d_skip[l], ssm_norm_w[l])
        p.update(norm_mix_w=norm_mix_w[l], norm_ffn_w=norm_ffn_w[l], q_norm_w=q_norm_w[l],
                 k_norm_w=k_norm_w[l], conv_w=conv_w[l], expert_u=expert_u[l], expert_v=expert_v[l])
        mod = _ada(c_all, w_ada[l], b_ada[l])
        mods = jnp.split(mod, 6, axis=1)
        ck = cache_k[l].reshape(bs, -1, D_KV)
        cv = cache_v[l].reshape(bs, -1, D_KV)

        def attend_s(q, k, proj, ck=ck, cv=cv):
            return _attn_sample(q, k, proj, ck, cv, ls)
        attend_s.q_dtype = f32

        y_p, k_p, v_p, conv_p, h_p = _layer(y_p, [m[:bp] for m in mods], pos_p, attend_p, zeros_conv,
                                            None, ROWS, p)
        y_s, k_s, v_s, conv_s, h_s = _layer(y_s, [m[bp:n_c] for m in mods], pos_s, attend_s,
                                            state_conv[l], state_ssm[l], ls, p)
        for lst, val in zip(outs, (k_p[:, seq - tail:], v_p[:, seq - tail:], k_s, v_s,
                                   h_p[None], h_s, conv_p, conv_s)):
            lst.append(val)
    return (y_p, y_s) + tuple(jnp.stack(o) for o in outs)
```
